```python
import math
import jax, jax.numpy as jnp
from jax import lax
import numpy as np

D_MODEL = 2048
BATCH = 8
SEQ = 2048
DEPTH = 2

N_A_LAYERS = DEPTH // 2
N_B_LAYERS = DEPTH - N_A_LAYERS
SSM_EXPAND = 2
SSM_D_INNER = SSM_EXPAND * D_MODEL
SSM_HEAD_DIM = 64
SSM_HEADS = SSM_D_INNER // SSM_HEAD_DIM
SSM_GROUPS = 8
SSM_STATE = 128
SSM_CONV = 4
SSM_CHUNK = 128
SSM_CONV_DIM = SSM_D_INNER + 2 * SSM_GROUPS * SSM_STATE
SSM_IN_DIM = 2 * SSM_D_INNER + 2 * SSM_GROUPS * SSM_STATE + SSM_HEADS
DIFF_HEADS = 8
DIFF_HEAD_DIM = D_MODEL // DIFF_HEADS // 2
Q_BLOCK = 128
REL_BUCKETS = 32
REL_MAX_DIST = 128
N_EXPERTS = 32
TOP_K = 4
EXPERT_FF = D_MODEL
SWIGLU_LIMIT = 7.0
SWIGLU_ALPHA = 1.702
EPS = 1e-6

kernel_name = 'hybrid_ssd_diffattn_moe'


def rms_norm(x, g):
    xf = x.astype(jnp.float32)
    y = xf * lax.rsqrt(jnp.mean(xf * xf, axis=-1, keepdims=True) + EPS)
    return (y * g.astype(jnp.float32)).astype(x.dtype)


def ada_mod(c, w, b, n):
    m = jax.nn.silu(c) @ w + b
    return [t[:, None, :] for t in jnp.split(m, n, axis=-1)]


def causal_depthwise_conv(u, w, b):
    K = w.shape[0]
    S = u.shape[1]
    up = jnp.pad(u, ((0, 0), (K - 1, 0), (0, 0)))
    out = up[:, 0:S] * w[0]
    for k in range(1, K):
        out = out + up[:, k:k + S] * w[k]
    return out + b


def ssd_chunked(xdt, a, bm, cm):
    Bsz, S, H, P = xdt.shape
    G, N = bm.shape[2], bm.shape[3]
    R = H // G
    L = SSM_CHUNK
    nc = S // L
    xc = xdt.reshape(Bsz, nc, L, G, R, P)
    bc = bm.reshape(Bsz, nc, L, G, N)
    cc = cm.reshape(Bsz, nc, L, G, N)
    ac = a.reshape(Bsz, nc, L, G, R).transpose(0, 1, 3, 4, 2)
    a_cs = jnp.cumsum(ac, axis=-1)
    causal = jnp.tril(jnp.ones((L, L), dtype=bool))
    seg = a_cs[..., :, None] - a_cs[..., None, :]
    decay_in = jnp.where(causal, jnp.exp(jnp.where(causal, seg, 0.0)), 0.0)
    cb = jnp.einsum('bclgn,bcsgn->bcgls', cc, bc)
    y_diag = jnp.einsum('bcgls,bcgrls,bcsgrp->bclgrp', cb, decay_in, xc)
    decay_out = jnp.exp(a_cs[..., -1:] - a_cs)
    states = jnp.einsum('bclgn,bcgrl,bclgrp->bcgrpn', bc, decay_out, xc)
    chunk_decay = jnp.exp(a_cs[..., -1])

    def step(carry, inp):
        st, dec = inp
        return carry * dec[..., None, None] + st, carry

    init = jnp.zeros_like(states[:, 0])
    _, prev = lax.scan(step, init, (jnp.moveaxis(states, 1, 0), jnp.moveaxis(chunk_decay, 1, 0)))
    prev = jnp.moveaxis(prev, 0, 1)
    y_off = jnp.einsum('bclgn,bcgrpn,bcgrl->bclgrp', cc, prev, jnp.exp(a_cs))
    return (y_diag + y_off).reshape(Bsz, S, H, P)


def mamba2_mixer(h, w_in, conv_w, conv_b, dt_bias, a_log, d_skip, norm_g, w_out):
    Bsz, S, _ = h.shape
    f32 = jnp.float32
    zxbcdt = h @ w_in
    z, xbc, dt = jnp.split(zxbcdt, [SSM_D_INNER, SSM_D_INNER + SSM_CONV_DIM], axis=-1)
    xbc = jax.nn.silu(causal_depthwise_conv(xbc, conv_w, conv_b))
    xs, bm, cm = jnp.split(xbc, [SSM_D_INNER, SSM_D_INNER + SSM_GROUPS * SSM_STATE], axis=-1)
    xs = xs.reshape(Bsz, S, SSM_HEADS, SSM_HEAD_DIM).astype(f32)
    bm = bm.reshape(Bsz, S, SSM_GROUPS, SSM_STATE).astype(f32)
    cm = cm.reshape(Bsz, S, SSM_GROUPS, SSM_STATE).astype(f32)
    dt = jax.nn.softplus(dt.astype(f32) + dt_bias.astype(f32))
    a = -jnp.exp(a_log.astype(f32))
    y = ssd_chunked(xs * dt[..., None], dt * a, bm, cm)
    y = y + xs * d_skip.astype(f32)[:, None]
    y = y.reshape(Bsz, S, SSM_D_INNER) * jax.nn.silu(z.astype(f32))
    yg = y.reshape(Bsz, S, SSM_GROUPS, SSM_D_INNER // SSM_GROUPS)
    yg = yg * lax.rsqrt(jnp.mean(yg * yg, axis=-1, keepdims=True) + EPS)
    y = yg.reshape(Bsz, S, SSM_D_INNER) * norm_g.astype(f32)
    return y.astype(h.dtype) @ w_out


def t5_causal_bucket(dist):
    max_exact = REL_BUCKETS // 2
    d = jnp.maximum(dist, 0)
    df = jnp.maximum(d, 1).astype(jnp.float32)
    large = max_exact + (jnp.log(df / max_exact) / math.log(REL_MAX_DIST / max_exact)
                         * (REL_BUCKETS - max_exact)).astype(jnp.int32)
    large = jnp.minimum(large, REL_BUCKETS - 1)
    return jnp.where(d < max_exact, d, large)


def shared_kv(xn, w_kv):
    Bsz, S, _ = xn.shape
    k, v = jnp.split(xn @ w_kv, 2, axis=-1)
    k = k.reshape(Bsz, S, DIFF_HEADS, 2, DIFF_HEAD_DIM).transpose(0, 2, 3, 1, 4)
    v = v.reshape(Bsz, S, DIFF_HEADS, 2 * DIFF_HEAD_DIM).transpose(0, 2, 1, 3)
    return k, v


def diff_attention(h, k, v, w_q, lam_vecs, subnorm_g, rel_bias, w_out, lambda_init):
    Bsz, S, _ = h.shape
    f32 = jnp.float32
    nb = S // Q_BLOCK
    scale = DIFF_HEAD_DIM ** -0.5
    q = (h @ w_q).reshape(Bsz, nb, Q_BLOCK, DIFF_HEADS, 2, DIFF_HEAD_DIM).transpose(1, 0, 3, 4, 2, 5)
    lv = lam_vecs.astype(f32)
    lam = jnp.exp(jnp.sum(lv[0] * lv[1])) - jnp.exp(jnp.sum(lv[2] * lv[3])) + lambda_init
    k_pos = jnp.arange(S)
    vf = v.astype(f32)

    def block(args):
        qb, i = args
        q_pos = i * Q_BLOCK + jnp.arange(Q_BLOCK)
        dist = q_pos[:, None] - k_pos[None, :]
        bias = jnp.moveaxis(rel_bias[t5_causal_bucket(dist)].astype(f32), -1, 0)
        logits = jnp.einsum('bhiqd,bhikd->bhiqk', qb, k).astype(f32) * scale + bias[None, :, None]
        logits = jnp.where(dist >= 0, logits, -jnp.inf)
        p = jax.nn.softmax(logits, axis=-1)
        attn = p[:, :, 0] - lam * p[:, :, 1]
        return jnp.einsum('bhqk,bhkv->bhqv', attn, vf)

    o = lax.map(block, (q, jnp.arange(nb)))
    o = o.transpose(1, 0, 3, 2, 4).reshape(Bsz, S, DIFF_HEADS, 2 * DIFF_HEAD_DIM)
    o = rms_norm(o, subnorm_g) * (1.0 - lambda_init)
    return o.reshape(Bsz, S, D_MODEL).astype(h.dtype) @ w_out


def moe_ffn(h, w_r, b_r, w1, b1, w2, b2):
    Bsz, S, D = h.shape
    f32 = jnp.float32
    t = h.reshape(-1, D)
    logits = (t @ w_r + b_r).astype(f32)
    top_v, top_i = lax.top_k(logits, TOP_K)
    top_w = jax.nn.softmax(top_v, axis=-1)
    gates = jnp.einsum('tk,tke->te', top_w, jax.nn.one_hot(top_i, N_EXPERTS, dtype=f32))

    def expert(acc, params):
        w1_e, b1_e, w2_e, b2_e, g_e = params
        u = t @ w1_e + b1_e
        x_glu = jnp.minimum(u[:, 0::2], SWIGLU_LIMIT)
        x_lin = jnp.clip(u[:, 1::2], -SWIGLU_LIMIT, SWIGLU_LIMIT)
        act = x_glu * jax.nn.sigmoid(SWIGLU_ALPHA * x_glu) * (x_lin + 1.0)
        y = act @ w2_e + b2_e
        return acc + g_e[:, None] * y.astype(f32), None

    acc, _ = lax.scan(expert, jnp.zeros((t.shape[0], D), f32), (w1, b1, w2, b2, gates.T))
    return acc.reshape(Bsz, S, D).astype(h.dtype)


def setup_inputs(seed: int = 0) -> dict:
    key = jax.random.key(seed)
    ks = iter(jax.random.split(key, 40))
    f32 = jnp.float32

    def nrm(shape, s):
        return jax.random.normal(next(ks), shape, f32) * s

    def gain(shape):
        return 1.0 + nrm(shape, 0.05)

    D = D_MODEL
    na, nbl = N_A_LAYERS, N_B_LAYERS
    dt0 = jnp.exp(jax.random.uniform(next(ks), (na, SSM_HEADS), f32)
                  * (math.log(0.1) - math.log(0.001)) + math.log(0.001))
    return {
        'x': nrm((BATCH, SEQ, D), 1.0),
        'c': nrm((BATCH, D), 1.0),
        'mix_mod_w': nrm((DEPTH, D, 3 * D), 0.5 * D ** -0.5),
        'mix_mod_b': nrm((DEPTH, 3 * D), 0.02),
        'mix_norm': gain((DEPTH, D)),
        'ffn_mod_w': nrm((DEPTH, D, 3 * D), 0.5 * D ** -0.5),
        'ffn_mod_b': nrm((DEPTH, 3 * D), 0.02),
        'ffn_norm': gain((DEPTH, D)),
        'ssm_in_w': nrm((na, D, SSM_IN_DIM), D ** -0.5),
        'ssm_conv_w': nrm((na, SSM_CONV, SSM_CONV_DIM), SSM_CONV ** -0.5),
        'ssm_conv_b': nrm((na, SSM_CONV_DIM), 0.02),
        'ssm_dt_bias': dt0 + jnp.log(-jnp.expm1(-dt0)),
        'ssm_a_log': jnp.log(jax.random.uniform(next(ks), (na, SSM_HEADS), f32, 1.0, 16.0)),
        'ssm_d': gain((na, SSM_HEADS)),
        'ssm_norm': gain((na, SSM_D_INNER)),
        'ssm_out_w': nrm((na, SSM_D_INNER, D), SSM_D_INNER ** -0.5),
        'kv_mod_w': nrm((D, 2 * D), 0.5 * D ** -0.5),
        'kv_mod_b': nrm((2 * D,), 0.02),
        'kv_norm': gain((D,)),
        'kv_w': nrm((D, 2 * D), D ** -0.5),
        'attn_q_w': nrm((nbl, D, D), D ** -0.5),
        'attn_lambda': nrm((nbl, 4, DIFF_HEAD_DIM), 0.1),
        'attn_subnorm': gain((nbl, 2 * DIFF_HEAD_DIM)),
        'attn_out_w': nrm((nbl, D, D), D ** -0.5),
        'rel_bias': nrm((REL_BUCKETS, DIFF_HEADS), 0.5),
        'router_w': nrm((DEPTH, D, N_EXPERTS), D ** -0.5),
        'router_b': nrm((DEPTH, N_EXPERTS), 0.01),
        'moe_w1': nrm((DEPTH, N_EXPERTS, D, 2 * EXPERT_FF), D ** -0.5),
        'moe_b1': nrm((DEPTH, N_EXPERTS, 2 * EXPERT_FF), 0.01),
        'moe_w2': nrm((DEPTH, N_EXPERTS, EXPERT_FF, D), EXPERT_FF ** -0.5),
        'moe_b2': nrm((DEPTH, N_EXPERTS, D), 0.01),
        'final_norm': gain((D,)),
    }


def reference(x, c, mix_mod_w, mix_mod_b, mix_norm, ffn_mod_w, ffn_mod_b, ffn_norm,
              ssm_in_w, ssm_conv_w, ssm_conv_b, ssm_dt_bias, ssm_a_log, ssm_d, ssm_norm, ssm_out_w,
              kv_mod_w, kv_mod_b, kv_norm, kv_w,
              attn_q_w, attn_lambda, attn_subnorm, attn_out_w, rel_bias,
              router_w, router_b, moe_w1, moe_b1, moe_w2, moe_b2, final_norm):
    k_sh = None
    v_sh = None
    for l in range(DEPTH):
        shift, scale, gate = ada_mod(c, mix_mod_w[l], mix_mod_b[l], 3)
        hn = rms_norm(x, mix_norm[l]) * (1.0 + scale) + shift
        if l < N_A_LAYERS:
            a = l
            y = mamba2_mixer(hn, ssm_in_w[a], ssm_conv_w[a], ssm_conv_b[a], ssm_dt_bias[a],
                             ssm_a_log[a], ssm_d[a], ssm_norm[a], ssm_out_w[a])
        else:
            if l == N_A_LAYERS:
                kv_shift, kv_scale = ada_mod(c, kv_mod_w, kv_mod_b, 2)
                k_sh, v_sh = shared_kv(rms_norm(x, kv_norm) * (1.0 + kv_scale) + kv_shift, kv_w)
            bi = l - N_A_LAYERS
            lambda_init = 0.8 - 0.6 * math.exp(-0.3 * l)
            y = diff_attention(hn, k_sh, v_sh, attn_q_w[bi], attn_lambda[bi], attn_subnorm[bi],
                               rel_bias, attn_out_w[bi], lambda_init)
        x = x + gate * y
        shift, scale, gate = ada_mod(c, ffn_mod_w[l], ffn_mod_b[l], 3)
        hn = rms_norm(x, ffn_norm[l]) * (1.0 + scale) + shift
        x = x + gate * moe_ffn(hn, router_w[l], router_b[l], moe_w1[l], moe_b1[l], moe_w2[l], moe_b2[l])
    return rms_norm(x, final_norm)
```

```python
import functools
import math

import jax
import jax.numpy as jnp
from jax import lax
from jax.experimental import pallas as pl
from jax.experimental.pallas import tpu as pltpu

F32 = jnp.float32
BF16 = jnp.bfloat16
U32 = jnp.uint32
I32 = jnp.int32
HI = lax.Precision.HIGHEST

EPS = 1e-6
SSM_GROUPS = 8
SSM_STATE = 128
SSM_CHUNK = 128
TOP_K = 4
REL_BUCKETS = 32
REL_MAX_DIST = 128
SWIGLU_LIMIT = 7.0
SWIGLU_ALPHA = 1.702
N_A_LAYERS = 1

LANES = 128
PACK_ROWS = 8
VMEM_LIMIT = 56 * 1024 * 1024
MOE_TM = 256


def _cparams(sem):
    return pltpu.CompilerParams(dimension_semantics=sem, vmem_limit_bytes=VMEM_LIMIT)


def _mod_kernel(c_ref, w_ref, b_ref, o_ref):
    c = c_ref[...]
    s = c * jax.nn.sigmoid(c)
    o_ref[...] = jnp.dot(s, w_ref[...], precision=HI, preferred_element_type=F32) + b_ref[...]


def _ada_mod(c, w, b, layer):
    bsz, d = c.shape
    n = w.shape[-1]
    tn = 512
    out = pl.pallas_call(
        _mod_kernel,
        grid=(n // tn,),
        in_specs=[
            pl.BlockSpec((bsz, d), lambda j: (0, 0)),
            pl.BlockSpec((None, d, tn), lambda j: (layer, 0, j)),
            pl.BlockSpec((None, 1, tn), lambda j: (layer, 0, j)),
        ],
        out_specs=pl.BlockSpec((bsz, tn), lambda j: (0, j)),
        out_shape=jax.ShapeDtypeStruct((bsz, n), F32),
        compiler_params=_cparams(("arbitrary",)),
        name="ada_mod",
    )(c, w, b.reshape(b.shape[0], 1, n))
    return out.reshape(bsz, 1, n)


def _rms(x, g):
    ms = jnp.mean(x * x, axis=-1, keepdims=True)
    return x * lax.rsqrt(ms + EPS) * g


def _norm_kernel(*refs, has_mod):
    if has_mod:
        x_ref, g_ref, sh_ref, sc_ref, o_ref = refs
    else:
        x_ref, g_ref, o_ref = refs
    y = _rms(x_ref[0], g_ref[...])
    if has_mod:
        y = y * (1.0 + sc_ref[0]) + sh_ref[0]
    o_ref[0] = y.astype(o_ref.dtype)


def _norm(x3, g, layer, mod, out_dtype, ts=512):
    bsz, s, d = x3.shape
    ts = min(ts, s)
    has_mod = mod is not None
    in_specs = [
        pl.BlockSpec((1, ts, d), lambda b, i: (b, i, 0)),
        pl.BlockSpec((None, 1, d), lambda b, i: (layer, 0, 0)),
    ]
    args = [x3, g.reshape(g.shape[0], 1, d)]
    if has_mod:
        in_specs += [
            pl.BlockSpec((1, 1, d), lambda b, i: (b, 0, 0)),
            pl.BlockSpec((1, 1, d), lambda b, i: (b, 0, 1)),
        ]
        args += [mod, mod]
    return pl.pallas_call(
        functools.partial(_norm_kernel, has_mod=has_mod),
        grid=(bsz, s // ts),
        in_specs=in_specs,
        out_specs=pl.BlockSpec((1, ts, d), lambda b, i: (b, i, 0)),
        out_shape=jax.ShapeDtypeStruct((bsz, s, d), out_dtype),
        compiler_params=_cparams(("arbitrary", "arbitrary")),
        name="norm_mod",
    )(*args)


def _mm_kernel(*refs, has_res):
    if has_res:
        a_ref, w_ref, res_ref, gate_ref, o_ref, wb_ref = refs
    else:
        a_ref, w_ref, o_ref, wb_ref = refs

    @pl.when(pl.program_id(1) == 0)
    def _():
        wb_ref[...] = w_ref[...].astype(BF16)

    acc = jnp.dot(a_ref[...], wb_ref[...], preferred_element_type=F32)
    if has_res:
        o_ref[...] = res_ref[...] + gate_ref[0] * acc
    else:
        o_ref[...] = acc.astype(o_ref.dtype)


def _matmul(a, w, layer, col0, n, out_dtype, tm, tn, res=None, gate=None, gate_blk=0,
            rows_per_batch=None):
    m, k = a.shape
    tm = min(tm, m if rows_per_batch is None else rows_per_batch)
    cb0 = col0 // tn
    has_res = res is not None
    in_specs = [
        pl.BlockSpec((tm, k), lambda j, i: (i, 0)),
        pl.BlockSpec((None, k, tn), lambda j, i: (layer, 0, cb0 + j)),
    ]
    args = [a, w]
    if has_res:
        tiles_per_batch = rows_per_batch // tm
        gb0 = gate_blk * (n // tn)
        in_specs += [
            pl.BlockSpec((tm, tn), lambda j, i: (i, j)),
            pl.BlockSpec((1, 1, tn), lambda j, i: (i // tiles_per_batch, 0, gb0 + j)),
        ]
        args += [res, gate]
    return pl.pallas_call(
        functools.partial(_mm_kernel, has_res=has_res),
        grid=(n // tn, m // tm),
        in_specs=in_specs,
        out_specs=pl.BlockSpec((tm, tn), lambda j, i: (i, j)),
        out_shape=jax.ShapeDtypeStruct((m, n), out_dtype),
        scratch_shapes=[pltpu.VMEM((k, tn), BF16)],
        compiler_params=_cparams(("arbitrary", "arbitrary")),
        name="proj_matmul",
    )(*args)


def _softplus(x):
    return jnp.maximum(x, 0.0) + jnp.log(1.0 + jnp.exp(-jnp.abs(x)))


def _ssd_kernel(z_ref, x_ref, b_ref, c_ref, dt_ref, dtt_ref,
                cwx_ref, cwb_ref, cwc_ref, cbx_ref, cbb_ref, cbc_ref,
                dtb_ref, dtbt_ref, alog_ref, alogt_ref, dsk_ref, ng_ref,
                o_ref, state_ref, px_ref, pb_ref, pc_ref, pad_ref, *, heads_per_group, head_dim):
    L = x_ref.shape[1]
    R, P = heads_per_group, head_dim
    W = R * P

    @pl.when(pl.program_id(2) == 0)
    def _():
        state_ref[...] = jnp.zeros_like(state_ref)
        px_ref[...] = jnp.zeros_like(px_ref)
        pb_ref[...] = jnp.zeros_like(pb_ref)
        pc_ref[...] = jnp.zeros_like(pc_ref)

    def conv_silu(u_ref, prev_ref, w_ref, bias_ref):
        u = u_ref[0].astype(F32)
        prev = prev_ref[...]
        row = lax.broadcasted_iota(I32, u.shape, 0)
        kw = w_ref.shape[0]
        acc = u * w_ref[kw - 1:kw, :]
        for j in range(1, kw):
            sh = jnp.where(row < j, pltpu.roll(prev, j, 0), pltpu.roll(u, j, 0))
            acc = acc + sh * w_ref[kw - 1 - j:kw - j, :]
        acc = acc + bias_ref[...]
        prev_ref[...] = u
        return acc * jax.nn.sigmoid(acc)

    xs = conv_silu(x_ref, px_ref, cwx_ref, cbx_ref)
    bm = conv_silu(b_ref, pb_ref, cwb_ref, cbb_ref)
    cm = conv_silu(c_ref, pc_ref, cwc_ref, cbc_ref)

    dt = _softplus(dt_ref[0, 0] + dtb_ref[0])
    a = dt * (-jnp.exp(alog_ref[0]))
    dtt = _softplus(dtt_ref[0, 0] + dtbt_ref[0])
    at = dtt * (-jnp.exp(alogt_ref[0]))
    rowi = lax.broadcasted_iota(I32, (L, R), 0)
    coli = lax.broadcasted_iota(I32, (R, L), 1)
    acs, acst = a, at
    s = 1
    while s < L:
        acs = acs + jnp.where(rowi >= s, pltpu.roll(acs, s, 0), 0.0)
        acst = acst + jnp.where(coli >= s, pltpu.roll(acst, s, 1), 0.0)
        s *= 2

    pad_ref[...] = jnp.zeros_like(pad_ref)
    er = lax.broadcasted_iota(I32, (LANES, W), 0)
    ec = lax.broadcasted_iota(I32, (LANES, W), 1)
    expand_m = ((ec >= er * P) & (ec < (er + 1) * P)).astype(F32)

    def expand(v):
        pad_ref[:, 0:R] = v
        return jnp.dot(pad_ref[...], expand_m, precision=HI, preferred_element_type=F32)

    dte = expand(dt)
    acse = expand(acs)
    exp_acs = jnp.exp(acse)
    last = acse[L - 1:L, :]
    decay_out = jnp.exp(last - acse)
    chunk_decay = exp_acs[L - 1:L, :]

    xdt = xs * dte
    bmb = bm.astype(BF16)
    cmb = cm.astype(BF16)
    cb = lax.dot_general(cmb, bmb, (((1,), (1,)), ((), ())), preferred_element_type=F32)
    li = lax.broadcasted_iota(I32, (L, L), 0)
    si = lax.broadcasted_iota(I32, (L, L), 1)
    causal = li >= si
    xdtb = xdt.astype(BF16)
    ys = []
    for r in range(R):
        seg = acs[:, r:r + 1] - acst[r:r + 1, :]
        dec = jnp.exp(jnp.where(causal, seg, -jnp.inf))
        mm = (cb * dec).astype(BF16)
        ys.append(jnp.dot(mm, xdtb[:, r * P:(r + 1) * P], preferred_element_type=F32))
    y = jnp.concatenate(ys, axis=-1) if R > 1 else ys[0]

    prev = state_ref[...]
    y = y + jnp.dot(cmb, prev.astype(BF16), preferred_element_type=F32) * exp_acs
    st = lax.dot_general(bmb, (xdt * decay_out).astype(BF16), (((0,), (0,)), ((), ())),
                         preferred_element_type=F32)
    state_ref[...] = prev * chunk_decay + st

    y = y + xs * dsk_ref[...]
    zz = z_ref[0].astype(F32)
    y = y * (zz * jax.nn.sigmoid(zz))
    ms = jnp.mean(y * y, axis=-1, keepdims=True)
    o_ref[0] = (y * lax.rsqrt(ms + EPS) * ng_ref[...]).astype(o_ref.dtype)


def _ssd(zxbc, dt_raw, conv_w, conv_b, dt_bias, a_log, d_skip, norm_g, bsz, seq):
    G, N, L = SSM_GROUPS, SSM_STATE, SSM_CHUNK
    H = dt_raw.shape[-1]
    di = norm_g.shape[-1]
    P = di // H
    R = H // G
    W = R * P
    zx = zxbc.reshape(bsz, seq, zxbc.shape[-1])
    dt4 = dt_raw.reshape(bsz, seq, G, R).transpose(0, 2, 1, 3)
    dt4t = dt_raw.reshape(bsz, seq, G, R).transpose(0, 2, 3, 1)
    nbw = di // W
    nbn = di // N
    cw = conv_w
    cbias = conv_b
    kw = cw.shape[1]
    dtb = dt_bias.reshape(G, 1, R)
    dtbt = dt_bias.reshape(G, R, 1)
    al = a_log.reshape(G, 1, R)
    alt = a_log.reshape(G, R, 1)
    dsk = jnp.repeat(d_skip.reshape(H), P).reshape(1, di)
    grid = (bsz, G, seq // L)
    in_specs = [
        pl.BlockSpec((1, L, W), lambda b, g, c: (b, c, g)),
        pl.BlockSpec((1, L, W), lambda b, g, c: (b, c, nbw + g)),
        pl.BlockSpec((1, L, N), lambda b, g, c: (b, c, 2 * nbn + g)),
        pl.BlockSpec((1, L, N), lambda b, g, c: (b, c, 2 * nbn + G + g)),
        pl.BlockSpec((1, 1, L, R), lambda b, g, c: (b, g, c, 0)),
        pl.BlockSpec((1, 1, R, L), lambda b, g, c: (b, g, 0, c)),
        pl.BlockSpec((None, kw, W), lambda b, g, c: (0, 0, g)),
        pl.BlockSpec((None, kw, N), lambda b, g, c: (0, 0, nbn + g)),
        pl.BlockSpec((None, kw, N), lambda b, g, c: (0, 0, nbn + G + g)),
        pl.BlockSpec((1, W), lambda b, g, c: (0, g)),
        pl.BlockSpec((1, N), lambda b, g, c: (0, nbn + g)),
        pl.BlockSpec((1, N), lambda b, g, c: (0, nbn + G + g)),
        pl.BlockSpec((1, 1, R), lambda b, g, c: (g, 0, 0)),
        pl.BlockSpec((1, R, 1), lambda b, g, c: (g, 0, 0)),
        pl.BlockSpec((1, 1, R), lambda b, g, c: (g, 0, 0)),
        pl.BlockSpec((1, R, 1), lambda b, g, c: (g, 0, 0)),
        pl.BlockSpec((1, W), lambda b, g, c: (0, g)),
        pl.BlockSpec((1, W), lambda b, g, c: (0, g)),
    ]
    return pl.pallas_call(
        functools.partial(_ssd_kernel, heads_per_group=R, head_dim=P),
        grid=grid,
        in_specs=in_specs,
        out_specs=pl.BlockSpec((1, L, W), lambda b, g, c: (b, c, g)),
        out_shape=jax.ShapeDtypeStruct((bsz, seq, di), BF16),
        scratch_shapes=[
            pltpu.VMEM((N, W), F32),
            pltpu.VMEM((L, W), F32),
            pltpu.VMEM((L, N), F32),
            pltpu.VMEM((L, N), F32),
            pltpu.VMEM((L, LANES), F32),
        ],
        compiler_params=_cparams(("arbitrary", "arbitrary", "arbitrary")),
        name="ssd_mixer",
    )(zx, zx, zx, zx, dt4, dt4t, cw, cw, cw, cbias, cbias, cbias, dtb, dtbt, al, alt, dsk, norm_g)


def _bias_tile_kernel(rb_ref, o_ref, *, tq):
    h = pl.program_id(0)
    delta = pl.program_id(1)
    tk = o_ref.shape[-1]
    qi = lax.broadcasted_iota(I32, (tq, tk), 0)
    ki = lax.broadcasted_iota(I32, (tq, tk), 1)
    dist = delta * tq + qi - ki
    max_exact = REL_BUCKETS // 2
    d = jnp.maximum(dist, 0)
    df = jnp.maximum(d, 1).astype(F32)
    large = max_exact + (jnp.log(df / max_exact) / math.log(REL_MAX_DIST / max_exact)
                         * (REL_BUCKETS - max_exact)).astype(I32)
    large = jnp.minimum(large, REL_BUCKETS - 1)
    bucket = jnp.where(d < max_exact, d, large)
    val = jnp.zeros((tq, tk), F32)
    for j in range(REL_BUCKETS):
        val = jnp.where(bucket == j, rb_ref[j, h], val)
    o_ref[0, 0] = jnp.where(dist >= 0, val, -jnp.inf)


def _bias_tiles(rel_bias, tq):
    nh = rel_bias.shape[1]
    return pl.pallas_call(
        functools.partial(_bias_tile_kernel, tq=tq),
        grid=(nh, 3),
        in_specs=[pl.BlockSpec(memory_space=pltpu.SMEM)],
        out_specs=pl.BlockSpec((1, 1, tq, tq), lambda h, d: (h, d, 0, 0)),
        out_shape=jax.ShapeDtypeStruct((nh, 3, tq, tq), F32),
        compiler_params=_cparams(("arbitrary", "arbitrary")),
        name="rel_bias_tiles",
    )(rel_bias)


def _attn_kernel(q_ref, k_ref, v_ref, bias_ref, lam_ref, sg_ref, o_ref,
                 m1_ref, l1_ref, a1_ref, m2_ref, l2_ref, a2_ref, *, scale, lambda_init):
    i = pl.program_id(2)
    tq = q_ref.shape[1]
    dh = q_ref.shape[2] // 2
    q1 = q_ref[0, :, 0:dh]
    q2 = q_ref[0, :, dh:2 * dh]
    for m_ref, l_ref, a_ref in ((m1_ref, l1_ref, a1_ref), (m2_ref, l2_ref, a2_ref)):
        m_ref[...] = jnp.full_like(m_ref, -jnp.inf)
        l_ref[...] = jnp.zeros_like(l_ref)
        a_ref[...] = jnp.zeros_like(a_ref)

    def step(j, carry):
        off = pl.multiple_of(j * tq, tq)
        kk = k_ref[0, pl.ds(off, tq), :]
        vv = v_ref[0, pl.ds(off, tq), :]
        bias = bias_ref[0, jnp.minimum(i - j, 2)]
        for q, lo, m_ref, l_ref, a_ref in ((q1, 0, m1_ref, l1_ref, a1_ref),
                                           (q2, dh, m2_ref, l2_ref, a2_ref)):
            s = lax.dot_general(q, kk[:, lo:lo + dh], (((1,), (1,)), ((), ())),
                                preferred_element_type=F32) * scale + bias
            m_prev = m_ref[...]
            m_new = jnp.maximum(m_prev, jnp.max(s, axis=-1, keepdims=True))
            alpha = jnp.exp(m_prev - m_new)
            p = jnp.exp(s - m_new)
            l_ref[...] = alpha * l_ref[...] + jnp.sum(p, axis=-1, keepdims=True)
            a_ref[...] = alpha * a_ref[...] + jnp.dot(p.astype(BF16), vv, preferred_element_type=F32)
            m_ref[...] = m_new
        return carry

    lax.fori_loop(0, i + 1, step, 0)

    lv = lam_ref[...]
    lam = (jnp.exp(jnp.sum(lv[0:1] * lv[1:2], axis=-1, keepdims=True))
           - jnp.exp(jnp.sum(lv[2:3] * lv[3:4], axis=-1, keepdims=True)) + lambda_init)
    o = a1_ref[...] / l1_ref[...] - lam * (a2_ref[...] / l2_ref[...])
    o = _rms(o, sg_ref[...]) * (1.0 - lambda_init)
    o_ref[0] = o.astype(o_ref.dtype)


def _attention(q, kv, bias_t, lam_vecs, subnorm_g, bsz, seq, nh, lambda_init, tq):
    d = q.shape[-1]
    dv = d // nh
    dh = dv // 2
    q3 = q.reshape(bsz, seq, d)
    kv3 = kv.reshape(bsz, seq, 2 * d)
    kern = functools.partial(_attn_kernel, scale=dh ** -0.5, lambda_init=lambda_init)
    return pl.pallas_call(
        kern,
        grid=(bsz, nh, seq // tq),
        in_specs=[
            pl.BlockSpec((1, tq, dv), lambda b, h, i: (b, i, h)),
            pl.BlockSpec((1, seq, dv), lambda b, h, i: (b, 0, h)),
            pl.BlockSpec((1, seq, dv), lambda b, h, i: (b, 0, nh + h)),
            pl.BlockSpec((1, 3, tq, tq), lambda b, h, i: (h, 0, 0, 0)),
            pl.BlockSpec((None, 4, dh), lambda b, h, i: (0, 0, 0)),
            pl.BlockSpec((1, dv), lambda b, h, i: (0, 0)),
        ],
        out_specs=pl.BlockSpec((1, tq, dv), lambda b, h, i: (b, i, h)),
        out_shape=jax.ShapeDtypeStruct((bsz, seq, d), BF16),
        scratch_shapes=[
            pltpu.VMEM((tq, 1), F32), pltpu.VMEM((tq, 1), F32), pltpu.VMEM((tq, dv), F32),
            pltpu.VMEM((tq, 1), F32), pltpu.VMEM((tq, 1), F32), pltpu.VMEM((tq, dv), F32),
        ],
        compiler_params=_cparams(("arbitrary", "arbitrary", "arbitrary")),
        name="diff_attention",
    )(q3, kv3, kv3, bias_t, lam_vecs, subnorm_g)


def _bf16_bits(x):
    return pltpu.bitcast(x.astype(BF16).astype(F32), U32)


def _store_packed(o_ref, val):
    rows, d = val.shape
    nw = d // (2 * LANES)
    for j in range(nw):
        lo = _bf16_bits(val[:, (2 * j) * LANES:(2 * j + 1) * LANES])
        hi = _bf16_bits(val[:, (2 * j + 1) * LANES:(2 * j + 2) * LANES])
        o_ref[pl.ds(j, rows, stride=nw), :] = (lo >> 16) | (hi & jnp.uint32(0xFFFF0000))


def _load_packed(ref, rows, nw):
    parts = []
    for j in range(nw):
        w = ref[pl.ds(j, rows, stride=nw), :]
        parts.append(pltpu.bitcast(w << 16, F32))
        parts.append(pltpu.bitcast(w & jnp.uint32(0xFFFF0000), F32))
    return jnp.concatenate(parts, axis=-1)


def _router_kernel(x_ref, g_ref, sh_ref, sc_ref, wr_ref, br_ref, hp_ref, ti_ref, tw_ref):
    hn = _rms(x_ref[0], g_ref[...]) * (1.0 + sc_ref[0]) + sh_ref[0]
    _store_packed(hp_ref, hn)
    logits = lax.dot_general(wr_ref[...], hn, (((1,), (1,)), ((), ())),
                             precision=HI, preferred_element_type=F32) + br_ref[...]
    ne = logits.shape[0]
    idx = lax.broadcasted_iota(I32, logits.shape, 0)
    work = logits
    vals, ids = [], []
    for _ in range(TOP_K):
        m = jnp.max(work, axis=0, keepdims=True)
        sel = jnp.min(jnp.where(work == m, idx, ne), axis=0, keepdims=True)
        vals.append(m)
        ids.append(sel)
        work = jnp.where(idx == sel, -jnp.inf, work)
    es = [jnp.exp(v - vals[0]) for v in vals]
    tot = es[0] + es[1] + es[2] + es[3]
    ti_ref[...] = jnp.concatenate(ids, axis=0)
    tw_ref[...] = jnp.concatenate([e / tot for e in es], axis=0)


def _router(x3, g, layer, mod, wr_t, br, tr=256):
    bsz, s, d = x3.shape
    ne = wr_t.shape[0]
    nw = d // (2 * LANES)
    t = bsz * s
    spb = s // tr
    return pl.pallas_call(
        _router_kernel,
        grid=(bsz, spb),
        in_specs=[
            pl.BlockSpec((1, tr, d), lambda b, i: (b, i, 0)),
            pl.BlockSpec((None, 1, d), lambda b, i: (layer, 0, 0)),
            pl.BlockSpec((1, 1, d), lambda b, i: (b, 0, 0)),
            pl.BlockSpec((1, 1, d), lambda b, i: (b, 0, 1)),
            pl.BlockSpec((ne, d), lambda b, i: (0, 0)),
            pl.BlockSpec((ne, 1), lambda b, i: (0, 0)),
        ],
        out_specs=[
            pl.BlockSpec((tr * nw, LANES), lambda b, i: (b * spb + i, 0)),
            pl.BlockSpec((TOP_K, tr), lambda b, i: (0, b * spb + i)),
            pl.BlockSpec((TOP_K, tr), lambda b, i: (0, b * spb + i)),
        ],
        out_shape=[
            jax.ShapeDtypeStruct((t * nw, LANES), U32),
            jax.ShapeDtypeStruct((TOP_K, t), I32),
            jax.ShapeDtypeStruct((TOP_K, t), F32),
        ],
        compiler_params=_cparams(("arbitrary", "arbitrary")),
        name="router_topk",
    )(x3, g.reshape(g.shape[0], 1, d), mod, mod, wr_t, br)


def _gather_kernel(idx_ref, tab_ref, o_ref, buf_ref, sem, *, rows, nw):
    base = pl.program_id(0) * rows

    def issue(r, carry):
        src = pl.multiple_of(idx_ref[base + r] * nw, nw)
        dst = pl.multiple_of(r * nw, nw)
        pltpu.make_async_copy(tab_ref.at[pl.ds(src, nw), :], buf_ref.at[pl.ds(dst, nw), :], sem).start()
        return carry

    lax.fori_loop(0, rows, issue, 0)
    pltpu.make_async_copy(tab_ref.at[pl.ds(0, rows * nw), :], buf_ref, sem).wait()
    o_ref[...] = _load_packed(buf_ref, rows, nw).astype(o_ref.dtype)


def _gather_rows(table, n_rows, idx, rows=256):
    nw = table.shape[0] // n_rows
    n = idx.shape[0]
    return pl.pallas_call(
        functools.partial(_gather_kernel, rows=rows, nw=nw),
        grid_spec=pltpu.PrefetchScalarGridSpec(
            num_scalar_prefetch=1,
            grid=(n // rows,),
            in_specs=[pl.BlockSpec(memory_space=pl.ANY)],
            out_specs=pl.BlockSpec((rows, nw * 2 * LANES), lambda i, idx_ref: (i, 0)),
            scratch_shapes=[pltpu.VMEM((rows * nw, LANES), U32), pltpu.SemaphoreType.DMA],
        ),
        out_shape=jax.ShapeDtypeStruct((n, nw * 2 * LANES), BF16),
        compiler_params=_cparams(("arbitrary",)),
        name="gather_rows",
    )(idx, table)


def _glu_act(ug, ul):
    xg = jnp.minimum(ug, SWIGLU_LIMIT)
    xl = jnp.clip(ul, -SWIGLU_LIMIT, SWIGLU_LIMIT)
    return xg * jax.nn.sigmoid(SWIGLU_ALPHA * xg) * (xl + 1.0)


def _moe1_kernel(te_ref, tb_ref, tf_ref, x_ref, w_ref, bg_ref, bl_ref, o_ref, wg_ref, wl_ref):
    i = pl.program_id(1)
    flag = tf_ref[i]
    tn = w_ref.shape[-1]
    half = LANES

    @pl.when((flag & 1) == 1)
    def _():
        pr = lax.broadcasted_iota(I32, (2 * half, 2 * half), 0)
        pc = lax.broadcasted_iota(I32, (2 * half, 2 * half), 1)
        src_col = jnp.where(pc < half, 2 * pc, 2 * (pc - half) + 1)
        perm = jnp.where(pr == src_col, 1.0, 0.0).astype(BF16)
        for sb in range(tn // (2 * half)):
            blk = w_ref[:, sb * 2 * half:(sb + 1) * 2 * half].astype(BF16)
            de = jnp.dot(blk, perm, preferred_element_type=F32).astype(BF16)
            wg_ref[:, sb * half:(sb + 1) * half] = de[:, 0:half]
            wl_ref[:, sb * half:(sb + 1) * half] = de[:, half:2 * half]

    @pl.when((flag & 2) == 2)
    def _():
        x = x_ref[...]
        ug = jnp.dot(x, wg_ref[...], preferred_element_type=F32) + bg_ref[...]
        ul = jnp.dot(x, wl_ref[...], preferred_element_type=F32) + bl_ref[...]
        o_ref[...] = _glu_act(ug, ul).astype(o_ref.dtype)

    @pl.when((flag & 2) == 0)
    def _():
        o_ref[...] = jnp.zeros_like(o_ref)


def _moe_stage1(xs, w1, b1g, b1l, layer, tile_e, tile_blk, tile_flag, tn=1024):
    tp, d = xs.shape
    f2 = w1.shape[-1]
    tm = MOE_TM
    nt = tile_e.shape[0]
    hn = tn // 2
    return pl.pallas_call(
        _moe1_kernel,
        grid_spec=pltpu.PrefetchScalarGridSpec(
            num_scalar_prefetch=3,
            grid=(f2 // tn, nt),
            in_specs=[
                pl.BlockSpec((tm, d), lambda j, i, te, tb, tf: (tb[i], 0)),
                pl.BlockSpec((None, None, d, tn), lambda j, i, te, tb, tf: (layer, te[i], 0, j)),
                pl.BlockSpec((None, None, 1, hn), lambda j, i, te, tb, tf: (layer, te[i], 0, j)),
                pl.BlockSpec((None, None, 1, hn), lambda j, i, te, tb, tf: (layer, te[i], 0, j)),
            ],
            out_specs=pl.BlockSpec((tm, hn), lambda j, i, te, tb, tf: (tb[i], j)),
            scratch_shapes=[pltpu.VMEM((d, hn), BF16), pltpu.VMEM((d, hn), BF16)],
        ),
        out_shape=jax.ShapeDtypeStruct((tp, f2 // 2), BF16),
        compiler_params=_cparams(("arbitrary", "arbitrary")),
        name="moe_w1_glu",
    )(tile_e, tile_blk, tile_flag, xs, w1, b1g, b1l)


def _moe2_kernel(te_ref, tb_ref, tf_ref, h_ref, w_ref, b_ref, g_ref, o_ref, wb_ref):
    i = pl.program_id(1)
    flag = tf_ref[i]

    @pl.when((flag & 1) == 1)
    def _():
        wb_ref[...] = w_ref[...].astype(BF16)

    @pl.when((flag & 2) == 2)
    def _():
        y = jnp.dot(h_ref[...], wb_ref[...], preferred_element_type=F32) + b_ref[...]
        _store_packed(o_ref, y * g_ref[...])

    @pl.when((flag & 2) == 0)
    def _():
        o_ref[...] = jnp.zeros_like(o_ref)


def _moe_stage2(h, w2, b2, row_gate, layer, tile_e, tile_blk, tile_flag):
    tp, f = h.shape
    d = w2.shape[-1]
    tm = MOE_TM
    nt = tile_e.shape[0]
    nw = d // (2 * LANES)
    ne = w2.shape[1]
    return pl.pallas_call(
        _moe2_kernel,
        grid_spec=pltpu.PrefetchScalarGridSpec(
            num_scalar_prefetch=3,
            grid=(1, nt),
            in_specs=[
                pl.BlockSpec((tm, f), lambda j, i, te, tb, tf: (tb[i], 0)),
                pl.BlockSpec((None, None, f, d), lambda j, i, te, tb, tf: (layer, te[i], 0, 0)),
                pl.BlockSpec((None, None, 1, d), lambda j, i, te, tb, tf: (layer, te[i], 0, 0)),
                pl.BlockSpec((tm, 1), lambda j, i, te, tb, tf: (tb[i], 0)),
            ],
            out_specs=pl.BlockSpec((tm * nw, LANES), lambda j, i, te, tb, tf: (tb[i], 0)),
            scratch_shapes=[pltpu.VMEM((f, d), BF16)],
        ),
        out_shape=jax.ShapeDtypeStruct((tp * nw, LANES), U32),
        compiler_params=_cparams(("arbitrary", "arbitrary")),
        name="moe_w2",
    )(tile_e, tile_blk, tile_flag, h, w2, b2.reshape(b2.shape[0], ne, 1, d), row_gate)


def _combine_kernel(x_ref, y_ref, gate_ref, o_ref):
    acc = y_ref[0].astype(F32)
    for k in range(1, y_ref.shape[0]):
        acc = acc + y_ref[k].astype(F32)
    o_ref[0] = x_ref[0] + gate_ref[0] * acc


def _combine(x3, y4, mod, gate_blk, ts=256):
    bsz, s, d = x3.shape
    spb = s // ts
    kk = y4.shape[0]
    return pl.pallas_call(
        _combine_kernel,
        grid=(bsz, spb),
        in_specs=[
            pl.BlockSpec((1, ts, d), lambda b, i: (b, i, 0)),
            pl.BlockSpec((kk, ts, d), lambda b, i: (0, b * spb + i, 0)),
            pl.BlockSpec((1, 1, d), lambda b, i: (b, 0, gate_blk)),
        ],
        out_specs=pl.BlockSpec((1, ts, d), lambda b, i: (b, i, 0)),
        out_shape=jax.ShapeDtypeStruct((bsz, s, d), F32),
        compiler_params=_cparams(("arbitrary", "arbitrary")),
        name="moe_combine",
    )(x3, y4, mod)


def _route_plan(top_i, top_w, ne):
    kk, t = top_i.shape
    tm = MOE_TM
    n = kk * t
    nt = n // tm + ne
    tp = nt * tm
    e_flat = top_i.reshape(n)
    onehot = (e_flat[:, None] == jnp.arange(ne, dtype=I32)[None, :]).astype(I32)
    csum = jnp.cumsum(onehot, axis=0)
    cnt = csum[-1]
    rank = jnp.take_along_axis(csum, e_flat[:, None], axis=1)[:, 0] - 1
    tiles_e = (cnt + tm - 1) // tm
    tile_end = jnp.cumsum(tiles_e)
    tile_start = tile_end - tiles_e
    pos = tile_start[e_flat] * tm + rank
    tok = jnp.tile(jnp.arange(t, dtype=I32), kk)
    src_tok = jnp.zeros((tp,), I32).at[pos].set(tok, unique_indices=True)
    row_gate = jnp.zeros((tp,), F32).at[pos].set(top_w.reshape(n), unique_indices=True)
    n_valid = tile_end[-1]
    ti = jnp.arange(nt, dtype=I32)
    tile_e = jnp.minimum(jnp.searchsorted(tile_end, ti, side="right").astype(I32), ne - 1)
    valid = ti < n_valid
    last_e = jnp.max(jnp.where(tiles_e > 0, jnp.arange(ne, dtype=I32), 0))
    tile_e = jnp.where(valid, tile_e, last_e)
    tile_blk = ti
    prev_e = jnp.concatenate([jnp.full((1,), -1, I32), tile_e[:-1]])
    first = (tile_e != prev_e) & valid
    tile_flag = first.astype(I32) + 2 * valid.astype(I32)
    return src_tok, row_gate.reshape(tp, 1), pos, tile_e, tile_blk, tile_flag


def _moe_block(x3, mod, layer, ffn_norm, router_w, router_b, w1, b1, w2, b2):
    bsz, s, d = x3.shape
    ne = router_w.shape[-1]
    wr_t = router_w[layer].T
    br = router_b[layer].reshape(ne, 1)
    hp, top_i, top_w = _router(x3, ffn_norm, layer, mod, wr_t, br)
    src_tok, row_gate, pos, tile_e, tile_blk, tile_flag = _route_plan(top_i, top_w, ne)
    xs = _gather_rows(hp, bsz * s, src_tok)
    b1g = b1[:, :, None, 0::2]
    b1l = b1[:, :, None, 1::2]
    h = _moe_stage1(xs, w1, b1g, b1l, layer, tile_e, tile_blk, tile_flag)
    yp = _moe_stage2(h, w2, b2, row_gate, layer, tile_e, tile_blk, tile_flag)
    y4 = _gather_rows(yp, src_tok.shape[0], pos).reshape(TOP_K, bsz * s, d)
    return _combine(x3, y4, mod, 2)


def kernel(x, c, mix_mod_w, mix_mod_b, mix_norm, ffn_mod_w, ffn_mod_b, ffn_norm, ssm_in_w, ssm_conv_w, ssm_conv_b, ssm_dt_bias, ssm_a_log, ssm_d, ssm_norm, ssm_out_w, kv_mod_w, kv_mod_b, kv_norm, kv_w, attn_q_w, attn_lambda, attn_subnorm, attn_out_w, rel_bias, router_w, router_b, moe_w1, moe_b1, moe_w2, moe_b2, final_norm):
    bsz, seq, d = x.shape
    t = bsz * seq
    nh = rel_bias.shape[1]
    di = ssm_norm.shape[-1]
    gn2 = 2 * SSM_GROUPS * SSM_STATE
    n_heads = ssm_dt_bias.shape[-1]

    mod = _ada_mod(c, mix_mod_w, mix_mod_b, 0)
    hn = _norm(x, mix_norm, 0, mod, BF16).reshape(t, d)
    zxbc = _matmul(hn, ssm_in_w, 0, 0, 2 * di + gn2, BF16, tm=1024, tn=1024)
    dt_raw = _matmul(hn, ssm_in_w[:, :, 2 * di + gn2:], 0, 0, n_heads, F32, tm=1024, tn=n_heads)
    y = _ssd(zxbc, dt_raw, ssm_conv_w, ssm_conv_b, ssm_dt_bias, ssm_a_log, ssm_d, ssm_norm, bsz, seq)
    x = _matmul(y.reshape(t, di), ssm_out_w, 0, 0, d, F32, tm=1024, tn=512,
                res=x.reshape(t, d), gate=mod, gate_blk=2, rows_per_batch=seq).reshape(bsz, seq, d)
    mod = _ada_mod(c, ffn_mod_w, ffn_mod_b, 0)
    x = _moe_block(x, mod, 0, ffn_norm, router_w, router_b, moe_w1, moe_b1, moe_w2, moe_b2)

    kvm = _ada_mod(c, kv_mod_w.reshape(1, d, 2 * d), kv_mod_b.reshape(1, 2 * d), 0)
    xn = _norm(x, kv_norm.reshape(1, d), 0, kvm, BF16).reshape(t, d)
    kv = _matmul(xn, kv_w.reshape(1, d, 2 * d), 0, 0, 2 * d, BF16, tm=1024, tn=1024)

    lambda_init = 0.8 - 0.6 * math.exp(-0.3 * 1)
    mod = _ada_mod(c, mix_mod_w, mix_mod_b, 1)
    hn = _norm(x, mix_norm, 1, mod, BF16).reshape(t, d)
    q = _matmul(hn, attn_q_w, 0, 0, d, BF16, tm=1024, tn=1024)
    tq = 256
    bias_t = _bias_tiles(rel_bias, tq)
    o = _attention(q, kv, bias_t, attn_lambda, attn_subnorm, bsz, seq, nh, lambda_init, tq)
    x = _matmul(o.reshape(t, d), attn_out_w, 0, 0, d, F32, tm=1024, tn=512,
                res=x.reshape(t, d), gate=mod, gate_blk=2, rows_per_batch=seq).reshape(bsz, seq, d)
    mod = _ada_mod(c, ffn_mod_w, ffn_mod_b, 1)
    x = _moe_block(x, mod, 1, ffn_norm, router_w, router_b, moe_w1, moe_b1, moe_w2, moe_b2)

    return _norm(x, final_norm.reshape(1, d), 0, None, F32)
```
